```python
import math
import jax, jax.numpy as jnp
from jax import lax
import numpy as np

D_MODEL = 2048
BATCH = 4
SEQ = 2048
DEPTH = 1
DEC_BATCH = 32
DEC_SEQ = 8
PAST_LEN = 8192
PAGE_SIZE = 128

D_ATT = D_MODEL // 2
D_HEAD = 128
N_HEADS_ATT = D_ATT // D_HEAD
D_SSM = D_MODEL - D_ATT
SSM_GROUP = 16
N_SSM_GROUPS = D_SSM // SSM_GROUP
SSM_STATE = 64
D_MIX = D_ATT + D_SSM
SB_BIAS_INIT = -8.0
N_KEYS = 128
N_EXPERTS = N_KEYS * N_KEYS
PEER_HEADS = 8
PEER_TOPK = 16
PEER_DKEY = 256
PLE_DIM = 256
Q_BLOCK = 128
PEER_BLOCK = 128
EPS = 1e-6

kernel_name = 'hybrid_stickbreak_s5_peer_step'


def rms_norm(x, g):
    x32 = x.astype(jnp.float32)
    y = x32 * lax.rsqrt(jnp.mean(x32 * x32, axis=-1, keepdims=True) + EPS)
    return (y * g.astype(jnp.float32)).astype(x.dtype)


def _sb_block(q_blk, q_pos, k, v, k_pos, bias):
    z = (jnp.einsum('bqhd,bkhd->bhqk', q_blk, k, preferred_element_type=jnp.float32) * (D_HEAD ** -0.5)
         + bias.astype(jnp.float32)[None, :, None, None])
    mask = (k_pos[None, :] < q_pos[:, None])[None, None]
    log_om = jnp.where(mask, jax.nn.log_sigmoid(-z), 0.0)
    rev = lax.cumsum(log_om, axis=3, reverse=True) - log_om
    w = jnp.where(mask, jnp.exp(jax.nn.log_sigmoid(z) + rev), 0.0)
    return jnp.einsum('bhqk,bkhd->bqhd', w, v.astype(jnp.float32))


def stick_breaking_attention(q, k, v, q_offset, bias):
    b, tq, h, dh = q.shape
    blk = min(Q_BLOCK, tq)
    nb = -(-tq // blk)
    pad = nb * blk - tq
    qp = jnp.pad(q, ((0, 0), (0, pad), (0, 0), (0, 0)))
    q_blocks = jnp.moveaxis(qp.reshape(b, nb, blk, h, dh), 1, 0)
    q_pos = (q_offset + jnp.arange(nb * blk, dtype=jnp.int32)).reshape(nb, blk)
    k_pos = jnp.arange(k.shape[1], dtype=jnp.int32)
    out = lax.map(lambda args: _sb_block(args[0], args[1], k, v, k_pos, bias), (q_blocks, q_pos))
    return jnp.moveaxis(out, 0, 1).reshape(b, nb * blk, h, dh)[:, :tq]


def _cmul(ar, ai, br, bi):
    return ar * br - ai * bi, ar * bi + ai * br


def s5_scan(u, h0_re, h0_im, a_re, a_im, log_step, b_re, b_im, c_re, c_im, d):
    f32 = jnp.float32
    u32 = u.astype(f32)
    a_re = a_re.astype(f32)
    a_im = a_im.astype(f32)
    dt = jnp.exp(log_step.astype(f32))[:, None]
    mag = jnp.exp(a_re * dt)
    ab_re, ab_im = mag * jnp.cos(a_im * dt), mag * jnp.sin(a_im * dt)
    den = a_re * a_re + a_im * a_im
    f_re, f_im = _cmul(ab_re - 1.0, ab_im, a_re / den, -a_im / den)
    bb_re, bb_im = _cmul(f_re[..., None], f_im[..., None], b_re.astype(f32), b_im.astype(f32))
    bu_re = jnp.einsum('gpc,blgc->blgp', bb_re, u32)
    bu_im = jnp.einsum('gpc,blgc->blgp', bb_im, u32)
    ar = jnp.broadcast_to(ab_re, bu_re.shape)
    ai = jnp.broadcast_to(ab_im, bu_re.shape)

    def combine(e1, e2):
        a1r, a1i, b1r, b1i = e1
        a2r, a2i, b2r, b2i = e2
        nar, nai = _cmul(a2r, a2i, a1r, a1i)
        nbr, nbi = _cmul(a2r, a2i, b1r, b1i)
        return nar, nai, nbr + b2r, nbi + b2i

    pr, pim, sr, si = lax.associative_scan(combine, (ar, ai, bu_re, bu_im), axis=1)
    cr, ci = _cmul(pr, pim, h0_re.astype(f32)[:, None], h0_im.astype(f32)[:, None])
    h_re = sr + cr
    h_im = si + ci
    y = (jnp.einsum('gcp,blgp->blgc', c_re.astype(f32), h_re)
         - jnp.einsum('gcp,blgp->blgc', c_im.astype(f32), h_im)
         + d.astype(f32) * u32)
    return y, h_re[:, -1], h_im[:, -1]


def peer_experts(c, idx, g, w_up, w_down):
    t = c.shape[0]
    blk = min(PEER_BLOCK, t)
    nb = -(-t // blk)
    pad = nb * blk - t
    c_p = jnp.pad(c, ((0, pad), (0, 0))).reshape(nb, blk, D_MODEL)
    i_p = jnp.pad(idx, ((0, pad), (0, 0), (0, 0))).reshape(nb, blk, PEER_HEADS, PEER_TOPK)
    g_p = jnp.pad(g, ((0, pad), (0, 0), (0, 0))).reshape(nb, blk, PEER_HEADS, PEER_TOPK)

    def one(args):
        cb, ib, gb = args
        act = jax.nn.gelu(jnp.einsum('thkd,td->thk', w_up[ib], cb, preferred_element_type=jnp.float32))
        return jnp.einsum('thk,thkd->td', gb * act, w_down[ib], preferred_element_type=jnp.float32)

    out = lax.map(one, (c_p, i_p, g_p))
    return out.reshape(nb * blk, D_MODEL)[:t]


def peer_ffn(c, w_q, subkeys, w_up, w_down):
    lead = c.shape[:-1]
    cf = c.reshape(-1, D_MODEL)
    t = cf.shape[0]
    half = PEER_DKEY // 2
    q = jnp.einsum('td,de->te', cf, w_q, preferred_element_type=jnp.float32).reshape(t, PEER_HEADS, PEER_DKEY)
    s1 = jnp.einsum('thd,nd->thn', q[..., :half], subkeys[0].astype(jnp.float32))
    s2 = jnp.einsum('thd,nd->thn', q[..., half:], subkeys[1].astype(jnp.float32))
    v1, i1 = lax.top_k(s1, PEER_TOPK)
    v2, i2 = lax.top_k(s2, PEER_TOPK)
    cand = (v1[..., :, None] + v2[..., None, :]).reshape(t, PEER_HEADS, PEER_TOPK * PEER_TOPK)
    cidx = (i1[..., :, None] * N_KEYS + i2[..., None, :]).reshape(t, PEER_HEADS, PEER_TOPK * PEER_TOPK)
    top, pos = lax.top_k(cand, PEER_TOPK)
    idx = jnp.take_along_axis(cidx, pos, axis=-1)
    g = jax.nn.softmax(top, axis=-1)
    out = peer_experts(cf, idx, g, w_up, w_down)
    return out.reshape(*lead, D_MODEL).astype(c.dtype)


def decoder_layer(h, p, past_k, past_v, h0_re, h0_im, lp):
    (g_mix, w_in, att_bias, a_re, a_im, log_step, b_re, b_im, c_re, c_im, d, w_glu, b_glu,
     g_att_out, g_ssm_out, w_out, g_ffn, w_peer_q, peer_subkeys, peer_up, peer_down,
     w_ple, g_ple, w_ple_gate) = lp
    bsz, t, _ = h.shape
    a = rms_norm(h, g_mix)
    proj = jnp.einsum('btd,de->bte', a, w_in)
    q = proj[..., :D_ATT].reshape(bsz, t, N_HEADS_ATT, D_HEAD)
    k = proj[..., D_ATT:2 * D_ATT].reshape(bsz, t, N_HEADS_ATT, D_HEAD)
    v = proj[..., 2 * D_ATT:3 * D_ATT].reshape(bsz, t, N_HEADS_ATT, D_HEAD)
    u = proj[..., 3 * D_ATT:].reshape(bsz, t, N_SSM_GROUPS, SSM_GROUP)
    if past_k is None:
        k_all, v_all, offset = k, v, 0
    else:
        k_all = jnp.concatenate([past_k.astype(k.dtype), k], axis=1)
        v_all = jnp.concatenate([past_v.astype(v.dtype), v], axis=1)
        offset = past_k.shape[1]
    att = stick_breaking_attention(q, k_all, v_all, offset, att_bias).reshape(bsz, t, D_ATT)
    y, hT_re, hT_im = s5_scan(u, h0_re, h0_im, a_re, a_im, log_step, b_re, b_im, c_re, c_im, d)
    z = jax.nn.gelu(y.reshape(bsz, t, D_SSM))
    ssm = z * jax.nn.sigmoid(jnp.einsum('bte,ef->btf', z, w_glu.astype(jnp.float32)) + b_glu.astype(jnp.float32))
    mixed = jnp.concatenate([rms_norm(att, g_att_out), rms_norm(ssm, g_ssm_out)], axis=-1).astype(h.dtype)
    h = h + jnp.einsum('bte,ed->btd', mixed, w_out).astype(h.dtype)
    h = h + peer_ffn(rms_norm(h, g_ffn), w_peer_q, peer_subkeys, peer_up, peer_down)
    gate = jax.nn.sigmoid(jnp.einsum('btd,de->bte', rms_norm(h, g_ple), w_ple_gate, preferred_element_type=jnp.float32))
    h = h + (jnp.einsum('btp,pd->btd', p, w_ple, preferred_element_type=jnp.float32) * gate).astype(h.dtype)
    return h, k, v, hT_re, hT_im


def setup_inputs(seed: int = 0) -> dict:
    key = jax.random.key(seed)
    ks = jax.random.split(key, 40)
    f32 = jnp.float32

    def nrm(k, shape, scale):
        return jax.random.normal(k, shape, f32) * scale

    def gain(k, shape):
        return 1.0 + 0.01 * jax.random.normal(k, shape, f32)

    n_pages = PAST_LEN // PAGE_SIZE
    n_used = DEC_BATCH * n_pages
    n_pool = n_used + max(1, n_used // 4)
    page_table = jax.random.permutation(ks[6], n_pool)[:n_used].reshape(DEC_BATCH, n_pages).astype(jnp.int32)
    G, P, C = N_SSM_GROUPS, SSM_STATE, SSM_GROUP
    a_im0 = math.pi * jnp.arange(P, dtype=f32)
    return {
        'x_prompt': nrm(ks[0], (BATCH, SEQ, D_MODEL), 1.0),
        'x_sample': nrm(ks[1], (DEC_BATCH, DEC_SEQ, D_MODEL), 1.0),
        'p_prompt': nrm(ks[2], (DEPTH, BATCH, SEQ, PLE_DIM), 1.0),
        'p_sample': nrm(ks[3], (DEPTH, DEC_BATCH, DEC_SEQ, PLE_DIM), 1.0),
        'cache_k': nrm(ks[4], (DEPTH, n_pool, PAGE_SIZE, N_HEADS_ATT, D_HEAD), 1.0),
        'cache_v': nrm(ks[5], (DEPTH, n_pool, PAGE_SIZE, N_HEADS_ATT, D_HEAD), 1.0),
        'state_ssm_re': nrm(ks[7], (DEPTH, DEC_BATCH, G, P), 0.3),
        'state_ssm_im': nrm(ks[8], (DEPTH, DEC_BATCH, G, P), 0.3),
        'page_table': page_table,
        'g_mix': gain(ks[9], (DEPTH, D_MODEL)),
        'w_in': nrm(ks[10], (DEPTH, D_MODEL, 3 * D_ATT + D_SSM), D_MODEL ** -0.5),
        'att_bias': SB_BIAS_INIT + nrm(ks[33], (DEPTH, N_HEADS_ATT), 0.5),
        'ssm_a_re': -0.5 + nrm(ks[11], (DEPTH, G, P), 0.01),
        'ssm_a_im': a_im0 + nrm(ks[12], (DEPTH, G, P), 0.01),
        'ssm_log_step': jax.random.uniform(ks[13], (DEPTH, G), f32, math.log(1e-3), math.log(1e-1)),
        'ssm_b_re': nrm(ks[14], (DEPTH, G, P, C), (2 * C) ** -0.5),
        'ssm_b_im': nrm(ks[15], (DEPTH, G, P, C), (2 * C) ** -0.5),
        'ssm_c_re': nrm(ks[16], (DEPTH, G, C, P), P ** -0.5),
        'ssm_c_im': nrm(ks[17], (DEPTH, G, C, P), P ** -0.5),
        'ssm_d': nrm(ks[18], (DEPTH, G, C), 1.0),
        'w_glu': nrm(ks[19], (DEPTH, D_SSM, D_SSM), D_SSM ** -0.5),
        'b_glu': nrm(ks[20], (DEPTH, D_SSM), 0.01),
        'g_att_out': gain(ks[21], (DEPTH, D_ATT)),
        'g_ssm_out': gain(ks[22], (DEPTH, D_SSM)),
        'w_out': nrm(ks[23], (DEPTH, D_MIX, D_MODEL), D_MIX ** -0.5),
        'g_ffn': gain(ks[24], (DEPTH, D_MODEL)),
        'w_peer_q': nrm(ks[25], (DEPTH, D_MODEL, PEER_HEADS * PEER_DKEY), D_MODEL ** -0.5),
        'peer_subkeys': nrm(ks[26], (DEPTH, 2, N_KEYS, PEER_DKEY // 2), (PEER_DKEY // 2) ** -0.5),
        'peer_up': nrm(ks[27], (DEPTH, N_EXPERTS, D_MODEL), D_MODEL ** -0.5),
        'peer_down': nrm(ks[28], (DEPTH, N_EXPERTS, D_MODEL), PEER_HEADS ** -0.5),
        'w_ple': nrm(ks[29], (DEPTH, PLE_DIM, D_MODEL), PLE_DIM ** -0.5),
        'g_ple': gain(ks[30], (DEPTH, D_MODEL)),
        'w_ple_gate': nrm(ks[31], (DEPTH, D_MODEL, D_MODEL), D_MODEL ** -0.5),
        'g_final': gain(ks[32], (D_MODEL,)),
    }


def reference(x_prompt, x_sample, p_prompt, p_sample, cache_k, cache_v, state_ssm_re, state_ssm_im,
              page_table, g_mix, w_in, att_bias, ssm_a_re, ssm_a_im, ssm_log_step, ssm_b_re, ssm_b_im,
              ssm_c_re, ssm_c_im, ssm_d, w_glu, b_glu, g_att_out, g_ssm_out, w_out, g_ffn,
              w_peer_q, peer_subkeys, peer_up, peer_down, w_ple, g_ple, w_ple_gate, g_final):
    n_seq, n_pages = page_table.shape
    past_len = n_pages * PAGE_SIZE
    hp, hs = x_prompt, x_sample
    zeros_state = jnp.zeros((x_prompt.shape[0], N_SSM_GROUPS, SSM_STATE), jnp.float32)
    kp_l, vp_l, rp_l, ip_l, ks_l, vs_l, rs_l, is_l = [], [], [], [], [], [], [], []
    for l in range(DEPTH):
        lp = (g_mix[l], w_in[l], att_bias[l], ssm_a_re[l], ssm_a_im[l], ssm_log_step[l], ssm_b_re[l],
              ssm_b_im[l], ssm_c_re[l], ssm_c_im[l], ssm_d[l], w_glu[l], b_glu[l], g_att_out[l],
              g_ssm_out[l], w_out[l], g_ffn[l], w_peer_q[l], peer_subkeys[l], peer_up[l], peer_down[l],
              w_ple[l], g_ple[l], w_ple_gate[l])
        hp, k_p, v_p, r_p, i_p = decoder_layer(hp, p_prompt[l], None, None, zeros_state, zeros_state, lp)
        past_k = cache_k[l][page_table].reshape(n_seq, past_len, N_HEADS_ATT, D_HEAD)
        past_v = cache_v[l][page_table].reshape(n_seq, past_len, N_HEADS_ATT, D_HEAD)
        hs, k_s, v_s, r_s, i_s = decoder_layer(hs, p_sample[l], past_k, past_v,
                                               state_ssm_re[l], state_ssm_im[l], lp)
        kp_l.append(k_p.astype(cache_k.dtype))
        vp_l.append(v_p.astype(cache_v.dtype))
        rp_l.append(r_p.astype(state_ssm_re.dtype))
        ip_l.append(i_p.astype(state_ssm_im.dtype))
        ks_l.append(k_s.astype(cache_k.dtype))
        vs_l.append(v_s.astype(cache_v.dtype))
        rs_l.append(r_s.astype(state_ssm_re.dtype))
        is_l.append(i_s.astype(state_ssm_im.dtype))
    y_prompt = rms_norm(hp, g_final)
    y_sample = rms_norm(hs, g_final)
    return (y_prompt, y_sample,
            jnp.stack(kp_l), jnp.stack(vp_l), jnp.stack(rp_l), jnp.stack(ip_l),
            jnp.stack(ks_l), jnp.stack(vs_l), jnp.stack(rs_l), jnp.stack(is_l))
```

```python
import functools
import math

import jax
import jax.numpy as jnp
from jax import lax
from jax.experimental import pallas as pl
from jax.experimental.pallas import tpu as pltpu

F32 = jnp.float32
BF16 = jnp.bfloat16
EPS = 1e-6
LANES = 128
SUBLANES = 8
D_HEAD = 128
SSM_GROUP = 16
SSM_STATE = 64
PEER_TOPK = 16
N_KEYS = 128
PAGE = 128
NEG_INF = float("-inf")
VMEM_LIMIT = 56 * 1024 * 1024


def _cparams(sem):
    return pltpu.CompilerParams(dimension_semantics=sem, vmem_limit_bytes=VMEM_LIMIT)


def _const_spec(shape):
    nd = len(shape)
    return pl.BlockSpec(shape, lambda *_: (0,) * nd, pipeline_mode=pl.Buffered(1))


def _rms(x, g):
    return x * lax.rsqrt(jnp.mean(x * x, axis=-1, keepdims=True) + EPS) * g


def _gelu(x):
    return 0.5 * x * (1.0 + jnp.tanh(math.sqrt(2.0 / math.pi) * (x + 0.044715 * (x * x * x))))


def _softplus(z):
    return jnp.maximum(z, 0.0) + jnp.log1p(jnp.exp(-jnp.abs(z)))


def _split_bf16(x):
    hi = x.astype(BF16)
    lo = (x - hi.astype(F32)).astype(BF16)
    return hi, lo


def _strict_lower_ones(n):
    r = lax.broadcasted_iota(jnp.int32, (n, n), 0)
    c = lax.broadcasted_iota(jnp.int32, (n, n), 1)
    return (r > c).astype(BF16)


def _in_proj_kernel(x_ref, g_ref, w_ref, *out_refs):
    a = _rms(x_ref[...], g_ref[...]).astype(BF16)
    off = 0
    for o in out_refs:
        n = o.shape[1]
        o[...] = jnp.dot(a, w_ref[:, off:off + n], preferred_element_type=F32)
        off += n


def in_proj(x, g, w_bf16, widths, tile):
    n, d = x.shape
    return pl.pallas_call(
        _in_proj_kernel,
        grid=(n // tile,),
        in_specs=[pl.BlockSpec((tile, d), lambda i: (i, 0)),
                  _const_spec((1, d)),
                  _const_spec(w_bf16.shape)],
        out_specs=[pl.BlockSpec((tile, wd), lambda i: (i, 0)) for wd in widths],
        out_shape=[jax.ShapeDtypeStruct((n, wd), F32) for wd in widths],
        compiler_params=_cparams(("parallel",)),
    )(x, g.reshape(1, d), w_bf16)


def _sb_prompt_kernel(bias_ref, q_ref, k_ref, v_ref, o_ref, *, blk, scale):
    h = pl.program_id(1)
    i = pl.program_id(2)
    q = q_ref[...].astype(BF16)
    bias = bias_ref[h]
    row = lax.broadcasted_iota(jnp.int32, (blk, blk), 0)
    col = lax.broadcasted_iota(jnp.int32, (blk, blk), 1)
    tri = _strict_lower_ones(blk)
    dh = q.shape[1]

    def body(jj, carry):
        acc, csum = carry
        j = i - jj
        start = pl.multiple_of(j * blk, blk)
        k = k_ref[pl.ds(start, blk), :].astype(BF16)
        v = v_ref[pl.ds(start, blk), :].astype(BF16)
        z = lax.dot_general(q, k, (((1,), (1,)), ((), ())), preferred_element_type=F32) * scale + bias
        sp = _softplus(z)
        mask = (j * blk + col) < (i * blk + row)
        log_om = jnp.where(mask, -sp, 0.0)
        hi, lo = _split_bf16(log_om)
        rev = (jnp.dot(hi, tri, preferred_element_type=F32)
               + jnp.dot(lo, tri, preferred_element_type=F32) + csum)
        w = jnp.where(mask, jnp.exp(z - sp + rev), 0.0)
        acc = acc + jnp.dot(w.astype(BF16), v, preferred_element_type=F32)
        csum = csum + jnp.sum(log_om, axis=1, keepdims=True)
        return acc, csum

    acc, _ = lax.fori_loop(0, i + 1, body,
                           (jnp.zeros((blk, dh), F32), jnp.zeros((blk, 1), F32)))
    o_ref[...] = acc


def sb_attention_prompt(q, k, v, bias, batch, blk):
    n, da = q.shape
    t = n // batch
    nh = da // D_HEAD
    nq = t // blk
    return pl.pallas_call(
        functools.partial(_sb_prompt_kernel, blk=blk, scale=D_HEAD ** -0.5),
        grid=(batch, nh, nq),
        in_specs=[pl.BlockSpec(memory_space=pltpu.SMEM),
                  pl.BlockSpec((blk, D_HEAD), lambda b, h, i: (b * nq + i, h)),
                  pl.BlockSpec((t, D_HEAD), lambda b, h, i: (b, h)),
                  pl.BlockSpec((t, D_HEAD), lambda b, h, i: (b, h))],
        out_specs=pl.BlockSpec((blk, D_HEAD), lambda b, h, i: (b * nq + i, h)),
        out_shape=jax.ShapeDtypeStruct((n, da), F32),
        compiler_params=_cparams(("parallel", "parallel", "arbitrary")),
    )(bias, q, k, v)


def _sb_sample_kernel(pt_ref, bias_ref, q_ref, kn_ref, vn_ref, *rest, n_pg, nh, scale, sub):
    k_refs = rest[:n_pg]
    v_refs = rest[n_pg:2 * n_pg]
    o_ref = rest[2 * n_pg]
    acc_ref, csum_ref = rest[2 * n_pg + 1:]
    s = pl.program_id(1)
    tq = q_ref.shape[0]
    rows = nh * tq
    bias = bias_ref[...]
    q_heads = [q_ref[:, h * D_HEAD:(h + 1) * D_HEAD].astype(BF16) for h in range(nh)]
    tri = _strict_lower_ones(sub)

    def scores(k_heads):
        z = jnp.concatenate(
            [lax.dot_general(q_heads[h], k_heads[h], (((1,), (1,)), ((), ())),
                             preferred_element_type=F32) for h in range(nh)], axis=0)
        return z * scale + bias

    def update(z, v_heads, mask):
        nk = z.shape[1]
        sp = _softplus(z)
        log_om = -sp if mask is None else jnp.where(mask, -sp, 0.0)
        lsz = z - sp
        c = csum_ref[...]
        w_parts = [None] * (nk // sub)
        for blk in reversed(range(nk // sub)):
            sl = slice(blk * sub, (blk + 1) * sub)
            hi, lo = _split_bf16(log_om[:, sl])
            rev = (jnp.dot(hi, tri, preferred_element_type=F32)
                   + jnp.dot(lo, tri, preferred_element_type=F32) + c)
            w = jnp.exp(lsz[:, sl] + rev)
            if mask is not None:
                w = jnp.where(mask[:, sl], w, 0.0)
            w_parts[blk] = w
            c = c + jnp.sum(log_om[:, sl], axis=1, keepdims=True)
        csum_ref[...] = c
        w = w_parts[0] if len(w_parts) == 1 else jnp.concatenate(w_parts, axis=1)
        for h in range(nh):
            hs = slice(h * tq, (h + 1) * tq)
            acc_ref[hs, :] += jnp.dot(w[hs, :].astype(BF16), v_heads[h], preferred_element_type=F32)

    @pl.when(s == 0)
    def _():
        acc_ref[...] = jnp.zeros_like(acc_ref)
        csum_ref[...] = jnp.zeros_like(csum_ref)
        pad = jnp.zeros((sub - tq, D_HEAD), BF16)
        kn = [jnp.concatenate([kn_ref[:, h * D_HEAD:(h + 1) * D_HEAD].astype(BF16), pad], axis=0)
              for h in range(nh)]
        vn = [jnp.concatenate([vn_ref[:, h * D_HEAD:(h + 1) * D_HEAD].astype(BF16), pad], axis=0)
              for h in range(nh)]
        r = lax.broadcasted_iota(jnp.int32, (tq, sub), 0)
        cidx = lax.broadcasted_iota(jnp.int32, (tq, sub), 1)
        mask = jnp.concatenate([cidx < r] * nh, axis=0)
        update(scores(kn), vn, mask)

    k_heads = [jnp.concatenate([kr[:, h, :] for kr in k_refs], axis=0).astype(BF16) for h in range(nh)]
    v_heads = [jnp.concatenate([vr[:, h, :] for vr in v_refs], axis=0).astype(BF16) for h in range(nh)]
    update(scores(k_heads), v_heads, None)

    @pl.when(s == pl.num_programs(1) - 1)
    def _():
        for h in range(nh):
            o_ref[:, h * D_HEAD:(h + 1) * D_HEAD] = acc_ref[h * tq:(h + 1) * tq, :]


def sb_attention_sample(q, k_new, v_new, cache_k, cache_v, layer, page_table, bias, n_pg):
    n_seq, n_pages = page_table.shape
    n, da = q.shape
    tq = n // n_seq
    nh = da // D_HEAD
    n_steps = n_pages // n_pg
    rows = nh * tq
    sub = 2 * PAGE if (n_pg * PAGE) % (2 * PAGE) == 0 else PAGE

    def page_spec(kk):
        return pl.BlockSpec((None, None, PAGE, nh, D_HEAD),
                            lambda b, s, pt: (layer, pt[b, (n_steps - 1 - s) * n_pg + kk], 0, 0, 0))

    tok_spec = pl.BlockSpec((tq, da), lambda b, s, pt: (b, 0))
    grid_spec = pltpu.PrefetchScalarGridSpec(
        num_scalar_prefetch=1,
        grid=(n_seq, n_steps),
        in_specs=[pl.BlockSpec((rows, 1), lambda b, s, pt: (0, 0)), tok_spec, tok_spec, tok_spec]
        + [page_spec(kk) for kk in range(n_pg)] * 2,
        out_specs=tok_spec,
        scratch_shapes=[pltpu.VMEM((rows, D_HEAD), F32), pltpu.VMEM((rows, 1), F32)],
    )
    bias_rows = jnp.repeat(bias.astype(F32), tq).reshape(rows, 1)
    return pl.pallas_call(
        functools.partial(_sb_sample_kernel, n_pg=n_pg, nh=nh, scale=D_HEAD ** -0.5, sub=sub),
        grid_spec=grid_spec,
        out_shape=jax.ShapeDtypeStruct((n, da), F32),
        compiler_params=_cparams(("parallel", "arbitrary")),
    )(page_table, bias_rows, q, k_new, v_new, *([cache_k] * n_pg), *([cache_v] * n_pg))


def _ssm_kernel(u_ref, h0r_ref, h0i_ref, bb_ref, cc_ref, pwr_ref, pwi_ref, d_ref,
                y_ref, hr_out, hi_out, sr_ref, si_ref, cr_ref, ci_ref):
    i = pl.program_id(2)
    tt = u_ref.shape[0]
    ns = sr_ref.shape[1]

    @pl.when(i == 0)
    def _():
        cr_ref[...] = h0r_ref[...]
        ci_ref[...] = h0i_ref[...]

    u = u_ref[...]
    bu = jnp.dot(u.astype(BF16), bb_ref[...], preferred_element_type=F32)
    pwr = pwr_ref[...]
    pwi = pwi_ref[...]
    if tt >= SUBLANES:
        xr = bu[:, :ns].reshape(tt // SUBLANES, SUBLANES, ns)
        xi = bu[:, ns:].reshape(tt // SUBLANES, SUBLANES, ns)
        sub = lax.broadcasted_iota(jnp.int32, xr.shape, 1)
        for k in (1, 2, 4):
            ar = pwr[k - 1:k, :]
            ai = pwi[k - 1:k, :]
            m = sub >= k
            pr = jnp.where(m, pltpu.roll(xr, k, axis=1), 0.0)
            pi = jnp.where(m, pltpu.roll(xi, k, axis=1), 0.0)
            xr, xi = xr + ar * pr - ai * pi, xi + ar * pi + ai * pr
        sr_ref[...] = xr.reshape(tt, ns)
        si_ref[...] = xi.reshape(tt, ns)

    def grp(gi, carry):
        cr, ci = carry
        r0 = pl.multiple_of(gi * SUBLANES, SUBLANES)
        hr = sr_ref[pl.ds(r0, SUBLANES), :] + pwr * cr - pwi * ci
        hi = si_ref[pl.ds(r0, SUBLANES), :] + pwr * ci + pwi * cr
        sr_ref[pl.ds(r0, SUBLANES), :] = hr
        si_ref[pl.ds(r0, SUBLANES), :] = hi
        return hr[SUBLANES - 1:, :], hi[SUBLANES - 1:, :]

    cr, ci = lax.fori_loop(0, tt // SUBLANES, grp, (cr_ref[...], ci_ref[...]))
    cr_ref[...] = cr
    ci_ref[...] = ci
    h = jnp.concatenate([sr_ref[...].astype(BF16), si_ref[...].astype(BF16)], axis=1)
    y_ref[...] = jnp.dot(h, cc_ref[...], preferred_element_type=F32) + d_ref[...] * u

    @pl.when(i == pl.num_programs(2) - 1)
    def _():
        hr_out[...] = cr
        hi_out[...] = ci


def _ssm_params(a_re, a_im, log_step, b_re, b_im, c_re, c_im, d, gs):
    g, p = a_re.shape
    c = b_re.shape[2]
    dt = jnp.exp(log_step.astype(F32))[:, None]
    a_re = a_re.astype(F32)
    a_im = a_im.astype(F32)
    mag = jnp.exp(a_re * dt)
    ab_re, ab_im = mag * jnp.cos(a_im * dt), mag * jnp.sin(a_im * dt)
    den = a_re * a_re + a_im * a_im
    ir, ii = a_re / den, -a_im / den
    f_re = (ab_re - 1.0) * ir - ab_im * ii
    f_im = (ab_re - 1.0) * ii + ab_im * ir
    bb_re = f_re[..., None] * b_re.astype(F32) - f_im[..., None] * b_im.astype(F32)
    bb_im = f_re[..., None] * b_im.astype(F32) + f_im[..., None] * b_re.astype(F32)
    nsl = g // gs
    eye = jnp.eye(gs, dtype=F32)

    def bd_in(x):
        x = x.reshape(nsl, gs, p, c)
        return jnp.einsum("sgpc,gh->sgchp", x, eye).reshape(nsl, gs * c, gs * p)

    def bd_out(x):
        x = x.reshape(nsl, gs, c, p)
        return jnp.einsum("sgcp,gh->sgphc", x, eye).reshape(nsl, gs * p, gs * c)

    bb = jnp.concatenate([bd_in(bb_re), bd_in(bb_im)], axis=2).astype(BF16)
    cc = jnp.concatenate([bd_out(c_re.astype(F32)), bd_out(-c_im.astype(F32))], axis=1).astype(BF16)
    pr, pi = [ab_re], [ab_im]
    for _ in range(SUBLANES - 1):
        pr, pi = (pr + [pr[-1] * ab_re - pi[-1] * ab_im], pi + [pr[-1] * ab_im + pi[-1] * ab_re])
    pw_re = jnp.stack(pr).reshape(SUBLANES, g * p)
    pw_im = jnp.stack(pi).reshape(SUBLANES, g * p)
    return bb, cc, pw_re, pw_im, d.astype(F32).reshape(1, g * c)


def ssm(u, h0_re, h0_im, params, batch, tile, gs):
    bb, cc, pw_re, pw_im, dvec = params
    n, du = u.shape
    l = n // batch
    nt = l // tile
    nsl = bb.shape[0]
    uw = bb.shape[1]
    ns = bb.shape[2] // 2
    gp = h0_re.shape[1]
    st_spec = pl.BlockSpec((None, 1, ns), lambda b, s, i: (b, 0, s))
    outs = pl.pallas_call(
        _ssm_kernel,
        grid=(batch, nsl, nt),
        in_specs=[pl.BlockSpec((tile, uw), lambda b, s, i: (b * nt + i, s)),
                  st_spec, st_spec,
                  pl.BlockSpec((None, uw, 2 * ns), lambda b, s, i: (s, 0, 0)),
                  pl.BlockSpec((None, 2 * ns, uw), lambda b, s, i: (s, 0, 0)),
                  pl.BlockSpec((SUBLANES, ns), lambda b, s, i: (0, s)),
                  pl.BlockSpec((SUBLANES, ns), lambda b, s, i: (0, s)),
                  pl.BlockSpec((1, uw), lambda b, s, i: (0, s))],
        out_specs=[pl.BlockSpec((tile, uw), lambda b, s, i: (b * nt + i, s)), st_spec, st_spec],
        out_shape=[jax.ShapeDtypeStruct((n, du), F32),
                   jax.ShapeDtypeStruct((batch, 1, gp), F32),
                   jax.ShapeDtypeStruct((batch, 1, gp), F32)],
        scratch_shapes=[pltpu.VMEM((tile, ns), F32), pltpu.VMEM((tile, ns), F32),
                        pltpu.VMEM((1, ns), F32), pltpu.VMEM((1, ns), F32)],
        compiler_params=_cparams(("parallel", "parallel", "arbitrary")),
    )(u, h0_re.reshape(batch, 1, gp), h0_im.reshape(batch, 1, gp), bb, cc, pw_re, pw_im, dvec)
    y, hr, hi = outs
    return y, hr.reshape(batch, gp), hi.reshape(batch, gp)


def _mix_out_kernel(att_ref, y_ref, x_ref, wglu_ref, bglu_ref, gatt_ref, gssm_ref, wout_ref, gffn_ref,
                    h1_ref, ct_ref):
    z = _gelu(y_ref[...])
    gate = jax.nn.sigmoid(jnp.dot(z.astype(BF16), wglu_ref[...], preferred_element_type=F32) + bglu_ref[...])
    ssm_o = z * gate
    ma = _rms(att_ref[...], gatt_ref[...]).astype(BF16)
    ms = _rms(ssm_o, gssm_ref[...]).astype(BF16)
    da = ma.shape[1]
    h1 = (x_ref[...] + jnp.dot(ma, wout_ref[:da, :], preferred_element_type=F32)
          + jnp.dot(ms, wout_ref[da:, :], preferred_element_type=F32))
    h1_ref[...] = h1
    ct_ref[...] = _rms(h1, gffn_ref[...]).T.astype(BF16)


def mix_out(att, y, x, wglu, bglu, gatt, gssm, wout, gffn, tile):
    n, d = x.shape
    da = att.shape[1]
    ds_ = y.shape[1]
    row = lambda w: pl.BlockSpec((tile, w), lambda i: (i, 0))
    return pl.pallas_call(
        _mix_out_kernel,
        grid=(n // tile,),
        in_specs=[row(da), row(ds_), row(d), _const_spec(wglu.shape), _const_spec((1, ds_)),
                  _const_spec((1, da)), _const_spec((1, ds_)), _const_spec(wout.shape), _const_spec((1, d))],
        out_specs=[row(d), pl.BlockSpec((d, tile), lambda i: (0, i))],
        out_shape=[jax.ShapeDtypeStruct((n, d), F32), jax.ShapeDtypeStruct((d, n), BF16)],
        compiler_params=_cparams(("parallel",)),
    )(att, y, x, wglu, bglu.reshape(1, ds_), gatt.reshape(1, da), gssm.reshape(1, ds_), wout,
      gffn.reshape(1, d))


def _cand_rows(r1):
    need = PEER_TOPK // (r1 + 1)
    return -(-need // SUBLANES) * SUBLANES


def _topk_rows(s_ref, vals_ref, rank_ref):
    n, tt = s_ref.shape
    rowi = lax.broadcasted_iota(jnp.int32, (n, tt), 0).astype(F32)
    rank_ref[...] = jnp.full((n, tt), float(PEER_TOPK), F32)

    def it(r, _):
        s = s_ref[...]
        m = jnp.max(s, axis=0, keepdims=True)
        idx = jnp.min(jnp.where(s == m, rowi, float(n)), axis=0, keepdims=True)
        hit = rowi == idx
        rank_ref[...] = jnp.where(hit, r.astype(F32), rank_ref[...])
        s_ref[...] = jnp.where(hit, NEG_INF, s)
        vals_ref[pl.ds(r, 1), :] = m
        return 0

    lax.fori_loop(0, PEER_TOPK, it, 0)


def _peer_route_kernel(ct_ref, wq_ref, sk_ref, r2_ref, e2_ref, na_ref, fa_ref,
                       qt_ref, s1_ref, s2_ref, w1_ref, w2_ref, rk1_ref, rk2_ref, v1_ref, v2_ref,
                       cand_ref, sel_ref, n_ref, *, n_heads):
    tt = ct_ref.shape[1]
    dk = N_KEYS
    qt_ref[...] = jnp.dot(wq_ref[...], ct_ref[...], preferred_element_type=F32)
    offs = [0]
    for r1 in range(PEER_TOPK):
        offs.append(offs[-1] + _cand_rows(r1))
    n_cand = offs[-1]
    crow = lax.broadcasted_iota(jnp.int32, (n_cand, tt), 0).astype(F32)

    def head(h, _):
        q0 = pl.multiple_of(h * 2 * dk, 2 * dk)
        o0 = pl.multiple_of(h * dk, dk)
        s1 = jnp.dot(sk_ref[0], qt_ref[pl.ds(q0, dk), :], preferred_element_type=F32,
                     precision=lax.Precision.HIGHEST)
        s2 = jnp.dot(sk_ref[1], qt_ref[pl.ds(q0 + dk, dk), :], preferred_element_type=F32,
                     precision=lax.Precision.HIGHEST)
        s1_ref[...] = s1
        s2_ref[...] = s2
        w1_ref[...] = s1
        w2_ref[...] = s2
        _topk_rows(w1_ref, v1_ref, rk1_ref)
        _topk_rows(w2_ref, v2_ref, rk2_ref)
        v2 = v2_ref[...]
        for r1 in range(PEER_TOPK):
            nr = offs[r1 + 1] - offs[r1]
            cand_ref[offs[r1]:offs[r1 + 1], :] = v1_ref[r1:r1 + 1, :] + v2[:nr, :]
        cand0 = cand_ref[...]
        sel_ref[...] = jnp.zeros((n_cand, tt), F32)

        def it(r, _):
            c = cand_ref[...]
            m = jnp.max(c, axis=0, keepdims=True)
            idx = jnp.min(jnp.where(c == m, crow, float(n_cand)), axis=0, keepdims=True)
            hit = crow == idx
            sel_ref[...] = jnp.where(hit, 1.0, sel_ref[...])
            cand_ref[...] = jnp.where(hit, NEG_INF, c)
            return 0

        lax.fori_loop(0, PEER_TOPK, it, 0)
        sel = sel_ref[...]
        m1 = v1_ref[0:1, :]
        m2 = v2_ref[0:1, :]
        zsum = jnp.sum(sel * jnp.exp(cand0 - (m1 + m2)), axis=0, keepdims=True)
        for r1 in range(PEER_TOPK):
            n_ref[r1:r1 + 1, :] = jnp.sum(sel[offs[r1]:offs[r1 + 1], :], axis=0, keepdims=True)
        rk1 = rk1_ref[...]
        na = jnp.zeros((dk, tt), F32)
        for r1 in range(PEER_TOPK):
            na = jnp.where(rk1 == float(r1), n_ref[r1:r1 + 1, :], na)
        r2_ref[pl.ds(o0, dk), :] = rk2_ref[...].astype(r2_ref.dtype)
        e2_ref[pl.ds(o0, dk), :] = jnp.exp(s2_ref[...] - m2).astype(e2_ref.dtype)
        na_ref[pl.ds(o0, dk), :] = na
        fa_ref[pl.ds(o0, dk), :] = jnp.exp(s1_ref[...] - m1) / zsum
        return 0

    lax.fori_loop(0, n_heads, head, 0)


def peer_route(ct, wq_t, subkeys, tile):
    d, n = ct.shape
    dq = wq_t.shape[0]
    n_heads = dq // (2 * N_KEYS)
    rows = n_heads * N_KEYS
    n_cand = sum(_cand_rows(r1) for r1 in range(PEER_TOPK))
    col = lambda r: pl.BlockSpec((r, tile), lambda i: (0, i))
    vm = lambda r: pltpu.VMEM((r, tile), F32)
    return pl.pallas_call(
        functools.partial(_peer_route_kernel, n_heads=n_heads),
        grid=(n // tile,),
        in_specs=[col(d), _const_spec(wq_t.shape), _const_spec(subkeys.shape)],
        out_specs=[col(rows)] * 4,
        out_shape=[jax.ShapeDtypeStruct((rows, n), BF16), jax.ShapeDtypeStruct((rows, n), BF16),
                   jax.ShapeDtypeStruct((rows, n), F32), jax.ShapeDtypeStruct((rows, n), F32)],
        scratch_shapes=[vm(dq), vm(N_KEYS), vm(N_KEYS), vm(N_KEYS), vm(N_KEYS), vm(N_KEYS), vm(N_KEYS),
                        vm(PEER_TOPK), vm(PEER_TOPK), vm(n_cand), vm(n_cand), vm(PEER_TOPK)],
        compiler_params=_cparams(("parallel",)),
    )(ct, wq_t, subkeys.astype(F32))


def _peer_dense_kernel(ct_ref, up_ref, dn_ref, r2_ref, e2_ref, na_ref, fa_ref, o_ref, acc_ref, h_ref,
                       *, n_heads):
    j = pl.program_id(1)
    te = up_ref.shape[0]
    dk = N_KEYS

    @pl.when(j == 0)
    def _():
        acc_ref[...] = jnp.zeros_like(acc_ref)

    act = jnp.dot(up_ref[...], ct_ref[...], preferred_element_type=F32)
    for k in range(te // dk):
        a = j * (te // dk) + k
        w = None
        for h in range(n_heads):
            n_row = na_ref[pl.ds(h * dk + a, 1), :]
            f_row = fa_ref[pl.ds(h * dk + a, 1), :]
            r2 = r2_ref[h * dk:(h + 1) * dk, :].astype(F32)
            e2 = e2_ref[h * dk:(h + 1) * dk, :].astype(F32)
            term = jnp.where(r2 < n_row, e2, 0.0) * f_row
            w = term if w is None else w + term
        h_ref[k * dk:(k + 1) * dk, :] = (w * _gelu(act[k * dk:(k + 1) * dk, :])).astype(BF16)
    acc_ref[...] += jnp.dot(dn_ref[...], h_ref[...], preferred_element_type=F32)

    @pl.when(j == pl.num_programs(1) - 1)
    def _():
        o_ref[...] = acc_ref[...].T


def peer_dense(ct, up_bf16, down_t_bf16, r2, e2, na, fa, tile, te):
    d, n = ct.shape
    ne = up_bf16.shape[0]
    rows = r2.shape[0]
    n_heads = rows // N_KEYS
    col = lambda r: pl.BlockSpec((r, tile), lambda i, j: (0, i))
    return pl.pallas_call(
        functools.partial(_peer_dense_kernel, n_heads=n_heads),
        grid=(n // tile, ne // te),
        in_specs=[col(d),
                  pl.BlockSpec((te, d), lambda i, j: (j, 0)),
                  pl.BlockSpec((d, te), lambda i, j: (0, j)),
                  col(rows), col(rows), col(rows), col(rows)],
        out_specs=pl.BlockSpec((tile, d), lambda i, j: (i, 0)),
        out_shape=jax.ShapeDtypeStruct((n, d), F32),
        scratch_shapes=[pltpu.VMEM((d, tile), F32), pltpu.VMEM((te, tile), BF16)],
        compiler_params=_cparams(("parallel", "arbitrary")),
    )(ct, up_bf16, down_t_bf16, r2, e2, na, fa)


def _ple_final_kernel(h1_ref, peer_ref, p_ref, gple_ref, wgate_ref, wple_ref, gfin_ref, y_ref):
    h2 = h1_ref[...] + peer_ref[...]
    r = _rms(h2, gple_ref[...]).astype(BF16)
    gate = jax.nn.sigmoid(jnp.dot(r, wgate_ref[...], preferred_element_type=F32))
    h3 = h2 + jnp.dot(p_ref[...].astype(BF16), wple_ref[...], preferred_element_type=F32) * gate
    y_ref[...] = _rms(h3, gfin_ref[...])


def ple_final(h1, peer, p, gple, wgate, wple, gfin, tile):
    n, d = h1.shape
    dp = p.shape[1]
    row = lambda w: pl.BlockSpec((tile, w), lambda i: (i, 0))
    return pl.pallas_call(
        _ple_final_kernel,
        grid=(n // tile,),
        in_specs=[row(d), row(d), row(dp), _const_spec((1, d)), _const_spec(wgate.shape),
                  _const_spec(wple.shape), _const_spec((1, d))],
        out_specs=row(d),
        out_shape=jax.ShapeDtypeStruct((n, d), F32),
        compiler_params=_cparams(("parallel",)),
    )(h1, peer, p, gple.reshape(1, d), wgate, wple, gfin.reshape(1, d))


def _pick_tile(n, pref):
    t = min(n, pref)
    while n % t:
        t //= 2
    return t


def _layer(x, p, w, att_fn, h0_re, h0_im, batch):
    n, d = x.shape
    tile = _pick_tile(n, 256)
    q, k, v, u = in_proj(x, w["g_mix"], w["w_in"], w["widths"], tile)
    att = att_fn(q, k, v)
    l = n // batch
    y, ht_re, ht_im = ssm(u, h0_re, h0_im, w["ssm"], batch, _pick_tile(l, 256), w["ssm_gs"])
    h1, ct = mix_out(att, y, x, w["w_glu"], w["b_glu"], w["g_att_out"], w["g_ssm_out"], w["w_out"],
                     w["g_ffn"], tile)
    r2, e2, na, fa = peer_route(ct, w["w_peer_q_t"], w["peer_subkeys"], tile)
    peer = peer_dense(ct, w["peer_up"], w["peer_down_t"], r2, e2, na, fa, _pick_tile(n, 512), 512)
    y_out = ple_final(h1, peer, p, w["g_ple"], w["w_ple_gate"], w["w_ple"], w["g_final"], tile)
    return y_out, k, v, ht_re, ht_im


def kernel(x_prompt, x_sample, p_prompt, p_sample, cache_k, cache_v, state_ssm_re, state_ssm_im, page_table, g_mix, w_in, att_bias, ssm_a_re, ssm_a_im, ssm_log_step, ssm_b_re, ssm_b_im, ssm_c_re, ssm_c_im, ssm_d, w_glu, b_glu, g_att_out, g_ssm_out, w_out, g_ffn, w_peer_q, peer_subkeys, peer_up, peer_down, w_ple, g_ple, w_ple_gate, g_final):
    depth = w_in.shape[0]
    assert depth == 1
    bsz, seq, d = x_prompt.shape
    n_seq, dec_seq, _ = x_sample.shape
    nh = cache_k.shape[3]
    d_att = nh * D_HEAD
    n_grp = ssm_a_re.shape[1]
    d_ssm = n_grp * SSM_GROUP
    gs = 16
    l = 0
    w = dict(
        g_mix=g_mix[l], w_in=w_in[l].astype(BF16), widths=(d_att, d_att, d_att, d_ssm),
        ssm=_ssm_params(ssm_a_re[l], ssm_a_im[l], ssm_log_step[l], ssm_b_re[l], ssm_b_im[l],
                        ssm_c_re[l], ssm_c_im[l], ssm_d[l], gs),
        ssm_gs=gs,
        w_glu=w_glu[l].astype(BF16), b_glu=b_glu[l], g_att_out=g_att_out[l], g_ssm_out=g_ssm_out[l],
        w_out=w_out[l].astype(BF16), g_ffn=g_ffn[l],
        w_peer_q_t=w_peer_q[l].T.astype(BF16), peer_subkeys=peer_subkeys[l],
        peer_up=peer_up[l].astype(BF16), peer_down_t=peer_down[l].T.astype(BF16),
        w_ple=w_ple[l].astype(BF16), g_ple=g_ple[l], w_ple_gate=w_ple_gate[l].astype(BF16),
        g_final=g_final,
    )
    bias = att_bias[l].astype(F32)

    zeros_state = jnp.zeros((bsz, n_grp * SSM_STATE), F32)
    att_p = lambda q, k, v: sb_attention_prompt(q, k, v, bias, bsz, _pick_tile(seq, 256))
    y_p, k_p, v_p, r_p, i_p = _layer(x_prompt.reshape(bsz * seq, d), p_prompt[l].reshape(bsz * seq, -1),
                                     w, att_p, zeros_state, zeros_state, bsz)

    att_s = lambda q, k, v: sb_attention_sample(q, k, v, cache_k, cache_v, l, page_table, bias,
                                                _pick_tile(page_table.shape[1], 8))
    y_s, k_s, v_s, r_s, i_s = _layer(x_sample.reshape(n_seq * dec_seq, d),
                                     p_sample[l].reshape(n_seq * dec_seq, -1), w, att_s,
                                     state_ssm_re[l].reshape(n_seq, -1), state_ssm_im[l].reshape(n_seq, -1),
                                     n_seq)

    kv = lambda a, b_, t: a.reshape(1, b_, t, nh, D_HEAD).astype(cache_k.dtype)
    st = lambda a, b_: a.reshape(1, b_, n_grp, SSM_STATE).astype(state_ssm_re.dtype)
    return (y_p.reshape(bsz, seq, d), y_s.reshape(n_seq, dec_seq, d),
            kv(k_p, bsz, seq), kv(v_p, bsz, seq), st(r_p, bsz), st(i_p, bsz),
            kv(k_s, n_seq, dec_seq), kv(v_s, n_seq, dec_seq), st(r_s, n_seq), st(i_s, n_seq))
```

```python
import functools
import math

import jax
import jax.numpy as jnp
from jax import lax
from jax.experimental import pallas as pl
from jax.experimental.pallas import tpu as pltpu

F32 = jnp.float32
BF16 = jnp.bfloat16
EPS = 1e-6
LANES = 128
SUBLANES = 8
BF16_ROWS = 16
D_HEAD = 128
SSM_GROUP = 16
SSM_STATE = 64
PEER_TOPK = 16
N_KEYS = 128
PAGE = 128
NEG_INF = float("-inf")
VMEM_LIMIT = 56 * 1024 * 1024


def _cparams(sem):
    return pltpu.CompilerParams(dimension_semantics=sem, vmem_limit_bytes=VMEM_LIMIT)


def _const_spec(shape):
    nd = len(shape)
    return pl.BlockSpec(shape, lambda *_: (0,) * nd, pipeline_mode=pl.Buffered(1))


def _rms(x, g):
    return x * lax.rsqrt(jnp.mean(x * x, axis=-1, keepdims=True) + EPS) * g


def _gelu(x):
    return 0.5 * x * (1.0 + jnp.tanh(math.sqrt(2.0 / math.pi) * (x + 0.044715 * (x * x * x))))


def _softplus(z):
    return jnp.maximum(z, 0.0) + jnp.log1p(jnp.exp(-jnp.abs(z)))


def _split_bf16(x):
    hi = x.astype(BF16)
    lo = (x - hi.astype(F32)).astype(BF16)
    return hi, lo


def _strict_lower_ones(n):
    r = lax.broadcasted_iota(jnp.int32, (n, n), 0)
    c = lax.broadcasted_iota(jnp.int32, (n, n), 1)
    return (r > c).astype(BF16)


def _in_proj_kernel(x_ref, g_ref, w_ref, *out_refs):
    a = _rms(x_ref[...], g_ref[...]).astype(BF16)
    off = 0
    for o in out_refs:
        n = o.shape[1]
        o[...] = jnp.dot(a, w_ref[:, off:off + n], preferred_element_type=F32)
        off += n


def in_proj(x, g, w_bf16, widths, tile):
    n, d = x.shape
    return pl.pallas_call(
        _in_proj_kernel,
        grid=(n // tile,),
        in_specs=[pl.BlockSpec((tile, d), lambda i: (i, 0)),
                  _const_spec((1, d)),
                  _const_spec(w_bf16.shape)],
        out_specs=[pl.BlockSpec((tile, wd), lambda i: (i, 0)) for wd in widths],
        out_shape=[jax.ShapeDtypeStruct((n, wd), F32) for wd in widths],
        compiler_params=_cparams(("parallel",)),
        name="in_proj",
    )(x, g.reshape(1, d), w_bf16)


def _sb_prompt_kernel(bias_ref, q_ref, k_ref, v_ref, o_ref, *, blk, scale):
    h = pl.program_id(1)
    i = pl.program_id(2)
    q = q_ref[...].astype(BF16)
    bias = bias_ref[h]
    row = lax.broadcasted_iota(jnp.int32, (blk, blk), 0)
    col = lax.broadcasted_iota(jnp.int32, (blk, blk), 1)
    tri = _strict_lower_ones(blk)
    dh = q.shape[1]

    def body(jj, carry):
        acc, csum = carry
        j = i - jj
        start = pl.multiple_of(j * blk, blk)
        k = k_ref[pl.ds(start, blk), :].astype(BF16)
        v = v_ref[pl.ds(start, blk), :].astype(BF16)
        z = lax.dot_general(q, k, (((1,), (1,)), ((), ())), preferred_element_type=F32) * scale + bias
        sp = _softplus(z)
        mask = (j * blk + col) < (i * blk + row)
        log_om = jnp.where(mask, -sp, 0.0)
        hi, lo = _split_bf16(log_om)
        rev = (jnp.dot(hi, tri, preferred_element_type=F32)
               + jnp.dot(lo, tri, preferred_element_type=F32) + csum)
        w = jnp.where(mask, jnp.exp(z - sp + rev), 0.0)
        acc = acc + jnp.dot(w.astype(BF16), v, preferred_element_type=F32)
        csum = csum + jnp.sum(log_om, axis=1, keepdims=True)
        return acc, csum

    acc, _ = lax.fori_loop(0, i + 1, body,
                           (jnp.zeros((blk, dh), F32), jnp.zeros((blk, 1), F32)))
    o_ref[...] = acc


def sb_attention_prompt(q, k, v, bias, batch, blk):
    n, da = q.shape
    t = n // batch
    nh = da // D_HEAD
    nq = t // blk
    return pl.pallas_call(
        functools.partial(_sb_prompt_kernel, blk=blk, scale=D_HEAD ** -0.5),
        grid=(batch, nh, nq),
        in_specs=[pl.BlockSpec(memory_space=pltpu.SMEM),
                  pl.BlockSpec((blk, D_HEAD), lambda b, h, i: (b * nq + i, h)),
                  pl.BlockSpec((t, D_HEAD), lambda b, h, i: (b, h)),
                  pl.BlockSpec((t, D_HEAD), lambda b, h, i: (b, h))],
        out_specs=pl.BlockSpec((blk, D_HEAD), lambda b, h, i: (b * nq + i, h)),
        out_shape=jax.ShapeDtypeStruct((n, da), F32),
        compiler_params=_cparams(("parallel", "parallel", "arbitrary")),
        name="sb_attention_prompt",
    )(bias, q, k, v)


def _sb_sample_kernel(pt_ref, bias_ref, q_ref, kn_ref, vn_ref, *rest, n_pg, nh, scale, sub):
    k_refs = rest[:n_pg]
    v_refs = rest[n_pg:2 * n_pg]
    o_ref = rest[2 * n_pg]
    acc_ref, csum_ref = rest[2 * n_pg + 1:]
    s = pl.program_id(1)
    tq = q_ref.shape[0]
    rows = nh * tq
    bias = bias_ref[...]
    q_heads = [q_ref[:, h * D_HEAD:(h + 1) * D_HEAD].astype(BF16) for h in range(nh)]
    tri = _strict_lower_ones(sub)

    def scores(k_heads):
        z = jnp.concatenate(
            [lax.dot_general(q_heads[h], k_heads[h], (((1,), (1,)), ((), ())),
                             preferred_element_type=F32) for h in range(nh)], axis=0)
        return z * scale + bias

    def update(z, v_heads, mask):
        nk = z.shape[1]
        sp = _softplus(z)
        log_om = -sp if mask is None else jnp.where(mask, -sp, 0.0)
        lsz = z - sp
        c = csum_ref[...]
        w_parts = [None] * (nk // sub)
        for blk in reversed(range(nk // sub)):
            sl = slice(blk * sub, (blk + 1) * sub)
            hi, lo = _split_bf16(log_om[:, sl])
            rev = (jnp.dot(hi, tri, preferred_element_type=F32)
                   + jnp.dot(lo, tri, preferred_element_type=F32) + c)
            w = jnp.exp(lsz[:, sl] + rev)
            if mask is not None:
                w = jnp.where(mask[:, sl], w, 0.0)
            w_parts[blk] = w
            c = c + jnp.sum(log_om[:, sl], axis=1, keepdims=True)
        csum_ref[...] = c
        w = w_parts[0] if len(w_parts) == 1 else jnp.concatenate(w_parts, axis=1)
        for h in range(nh):
            hs = slice(h * tq, (h + 1) * tq)
            acc_ref[hs, :] += jnp.dot(w[hs, :].astype(BF16), v_heads[h], preferred_element_type=F32)

    @pl.when(s == 0)
    def _():
        acc_ref[...] = jnp.zeros_like(acc_ref)
        csum_ref[...] = jnp.zeros_like(csum_ref)
        pad = jnp.zeros((sub - tq, D_HEAD), BF16)
        kn = [jnp.concatenate([kn_ref[:, h * D_HEAD:(h + 1) * D_HEAD].astype(BF16), pad], axis=0)
              for h in range(nh)]
        vn = [jnp.concatenate([vn_ref[:, h * D_HEAD:(h + 1) * D_HEAD].astype(BF16), pad], axis=0)
              for h in range(nh)]
        r = lax.broadcasted_iota(jnp.int32, (tq, sub), 0)
        cidx = lax.broadcasted_iota(jnp.int32, (tq, sub), 1)
        mask = jnp.concatenate([cidx < r] * nh, axis=0)
        update(scores(kn), vn, mask)

    def head_rows(refs, h):
        return jnp.concatenate([r[pl.ds(h, PAGE, stride=nh), :] for r in refs], axis=0).astype(BF16)

    k_heads = [head_rows(k_refs, h) for h in range(nh)]
    v_heads = [head_rows(v_refs, h) for h in range(nh)]
    update(scores(k_heads), v_heads, None)

    @pl.when(s == pl.num_programs(1) - 1)
    def _():
        for h in range(nh):
            o_ref[:, h * D_HEAD:(h + 1) * D_HEAD] = acc_ref[h * tq:(h + 1) * tq, :]


def sb_attention_sample(q, k_new, v_new, cache_k, cache_v, layer, page_table, bias, n_pg):
    n_seq, n_pages = page_table.shape
    n, da = q.shape
    tq = n // n_seq
    nh = da // D_HEAD
    n_steps = n_pages // n_pg
    rows = nh * tq
    sub = 2 * PAGE if (n_pg * PAGE) % (2 * PAGE) == 0 else PAGE

    def page_spec(kk):
        return pl.BlockSpec((None, None, PAGE * nh, D_HEAD),
                            lambda b, s, pt: (layer, pt[b, (n_steps - 1 - s) * n_pg + kk], 0, 0))

    tok_spec = pl.BlockSpec((tq, da), lambda b, s, pt: (b, 0))
    grid_spec = pltpu.PrefetchScalarGridSpec(
        num_scalar_prefetch=1,
        grid=(n_seq, n_steps),
        in_specs=[pl.BlockSpec((rows, 1), lambda b, s, pt: (0, 0)), tok_spec, tok_spec, tok_spec]
        + [page_spec(kk) for kk in range(n_pg)] * 2,
        out_specs=tok_spec,
        scratch_shapes=[pltpu.VMEM((rows, D_HEAD), F32), pltpu.VMEM((rows, 1), F32)],
    )
    bias_rows = jnp.repeat(bias.astype(F32), tq).reshape(rows, 1)
    pool_shape = cache_k.shape[:2] + (PAGE * nh, D_HEAD)
    ck = cache_k.reshape(pool_shape)
    cv = cache_v.reshape(pool_shape)
    return pl.pallas_call(
        functools.partial(_sb_sample_kernel, n_pg=n_pg, nh=nh, scale=D_HEAD ** -0.5, sub=sub),
        grid_spec=grid_spec,
        out_shape=jax.ShapeDtypeStruct((n, da), F32),
        compiler_params=_cparams(("parallel", "arbitrary")),
        name="sb_attention_sample",
    )(page_table, bias_rows, q, k_new, v_new, *([ck] * n_pg), *([cv] * n_pg))


def _ssm_kernel(u_ref, h0r_ref, h0i_ref, bb_ref, cc_ref, pwr_ref, pwi_ref, d_ref,
                y_ref, hr_out, hi_out, sr_ref, si_ref, cr_ref, ci_ref):
    i = pl.program_id(2)
    tt = u_ref.shape[0]
    ns = sr_ref.shape[1]

    @pl.when(i == 0)
    def _():
        cr_ref[...] = h0r_ref[...]
        ci_ref[...] = h0i_ref[...]

    u = u_ref[...]
    bu = jnp.dot(u.astype(BF16), bb_ref[...], preferred_element_type=F32)
    pwr = pwr_ref[...]
    pwi = pwi_ref[...]
    if tt >= SUBLANES:
        xr = bu[:, :ns].reshape(tt // SUBLANES, SUBLANES, ns)
        xi = bu[:, ns:].reshape(tt // SUBLANES, SUBLANES, ns)
        sub = lax.broadcasted_iota(jnp.int32, xr.shape, 1)
        for k in (1, 2, 4):
            ar = pwr[k - 1:k, :]
            ai = pwi[k - 1:k, :]
            m = sub >= k
            pr = jnp.where(m, pltpu.roll(xr, k, axis=1), 0.0)
            pi = jnp.where(m, pltpu.roll(xi, k, axis=1), 0.0)
            xr, xi = xr + ar * pr - ai * pi, xi + ar * pi + ai * pr
        sr_ref[...] = xr.reshape(tt, ns)
        si_ref[...] = xi.reshape(tt, ns)

    def grp(gi, carry):
        cr, ci = carry
        r0 = pl.multiple_of(gi * SUBLANES, SUBLANES)
        hr = sr_ref[pl.ds(r0, SUBLANES), :] + pwr * cr - pwi * ci
        hi = si_ref[pl.ds(r0, SUBLANES), :] + pwr * ci + pwi * cr
        sr_ref[pl.ds(r0, SUBLANES), :] = hr
        si_ref[pl.ds(r0, SUBLANES), :] = hi
        return hr[SUBLANES - 1:, :], hi[SUBLANES - 1:, :]

    cr, ci = lax.fori_loop(0, tt // SUBLANES, grp, (cr_ref[...], ci_ref[...]))
    cr_ref[...] = cr
    ci_ref[...] = ci
    h = jnp.concatenate([sr_ref[...].astype(BF16), si_ref[...].astype(BF16)], axis=1)
    y_ref[...] = jnp.dot(h, cc_ref[...], preferred_element_type=F32) + d_ref[...] * u

    @pl.when(i == pl.num_programs(2) - 1)
    def _():
        hr_out[...] = cr
        hi_out[...] = ci


def _ssm_params(a_re, a_im, log_step, b_re, b_im, c_re, c_im, d, gs):
    g, p = a_re.shape
    c = b_re.shape[2]
    dt = jnp.exp(log_step.astype(F32))[:, None]
    a_re = a_re.astype(F32)
    a_im = a_im.astype(F32)
    mag = jnp.exp(a_re * dt)
    ab_re, ab_im = mag * jnp.cos(a_im * dt), mag * jnp.sin(a_im * dt)
    den = a_re * a_re + a_im * a_im
    ir, ii = a_re / den, -a_im / den
    f_re = (ab_re - 1.0) * ir - ab_im * ii
    f_im = (ab_re - 1.0) * ii + ab_im * ir
    bb_re = f_re[..., None] * b_re.astype(F32) - f_im[..., None] * b_im.astype(F32)
    bb_im = f_re[..., None] * b_im.astype(F32) + f_im[..., None] * b_re.astype(F32)
    nsl = g // gs
    eye = jnp.eye(gs, dtype=F32)

    def bd_in(x):
        x = x.reshape(nsl, gs, p, c)
        return jnp.einsum("sgpc,gh->sgchp", x, eye).reshape(nsl, gs * c, gs * p)

    def bd_out(x):
        x = x.reshape(nsl, gs, c, p)
        return jnp.einsum("sgcp,gh->sgphc", x, eye).reshape(nsl, gs * p, gs * c)

    bb = jnp.concatenate([bd_in(bb_re), bd_in(bb_im)], axis=2).astype(BF16)
    cc = jnp.concatenate([bd_out(c_re.astype(F32)), bd_out(-c_im.astype(F32))], axis=1).astype(BF16)
    pr, pi = [ab_re], [ab_im]
    for _ in range(SUBLANES - 1):
        pr, pi = (pr + [pr[-1] * ab_re - pi[-1] * ab_im], pi + [pr[-1] * ab_im + pi[-1] * ab_re])
    pw_re = jnp.stack(pr).reshape(SUBLANES, g * p)
    pw_im = jnp.stack(pi).reshape(SUBLANES, g * p)
    return bb, cc, pw_re, pw_im, d.astype(F32).reshape(1, g * c)


def ssm(u, h0_re, h0_im, params, batch, tile, gs):
    bb, cc, pw_re, pw_im, dvec = params
    n, du = u.shape
    l = n // batch
    nt = l // tile
    nsl = bb.shape[0]
    uw = bb.shape[1]
    ns = bb.shape[2] // 2
    gp = h0_re.shape[1]
    st_spec = pl.BlockSpec((None, 1, ns), lambda b, s, i: (b, 0, s))
    outs = pl.pallas_call(
        _ssm_kernel,
        grid=(batch, nsl, nt),
        in_specs=[pl.BlockSpec((tile, uw), lambda b, s, i: (b * nt + i, s)),
                  st_spec, st_spec,
                  pl.BlockSpec((None, uw, 2 * ns), lambda b, s, i: (s, 0, 0)),
                  pl.BlockSpec((None, 2 * ns, uw), lambda b, s, i: (s, 0, 0)),
                  pl.BlockSpec((SUBLANES, ns), lambda b, s, i: (0, s)),
                  pl.BlockSpec((SUBLANES, ns), lambda b, s, i: (0, s)),
                  pl.BlockSpec((1, uw), lambda b, s, i: (0, s))],
        out_specs=[pl.BlockSpec((tile, uw), lambda b, s, i: (b * nt + i, s)), st_spec, st_spec],
        out_shape=[jax.ShapeDtypeStruct((n, du), F32),
                   jax.ShapeDtypeStruct((batch, 1, gp), F32),
                   jax.ShapeDtypeStruct((batch, 1, gp), F32)],
        scratch_shapes=[pltpu.VMEM((tile, ns), F32), pltpu.VMEM((tile, ns), F32),
                        pltpu.VMEM((1, ns), F32), pltpu.VMEM((1, ns), F32)],
        compiler_params=_cparams(("parallel", "parallel", "arbitrary")),
        name="ssm",
    )(u, h0_re.reshape(batch, 1, gp), h0_im.reshape(batch, 1, gp), bb, cc, pw_re, pw_im, dvec)
    y, hr, hi = outs
    return y, hr.reshape(batch, gp), hi.reshape(batch, gp)


def _mix_out_kernel(att_ref, y_ref, x_ref, wglu_ref, bglu_ref, gatt_ref, gssm_ref, wout_ref, gffn_ref,
                    h1_ref, ct_ref):
    z = _gelu(y_ref[...])
    gate = jax.nn.sigmoid(jnp.dot(z.astype(BF16), wglu_ref[...], preferred_element_type=F32) + bglu_ref[...])
    ssm_o = z * gate
    ma = _rms(att_ref[...], gatt_ref[...]).astype(BF16)
    ms = _rms(ssm_o, gssm_ref[...]).astype(BF16)
    da = ma.shape[1]
    h1 = (x_ref[...] + jnp.dot(ma, wout_ref[:da, :], preferred_element_type=F32)
          + jnp.dot(ms, wout_ref[da:, :], preferred_element_type=F32))
    h1_ref[...] = h1
    ct_ref[...] = _rms(h1, gffn_ref[...]).T.astype(BF16)


def mix_out(att, y, x, wglu, bglu, gatt, gssm, wout, gffn, tile):
    n, d = x.shape
    da = att.shape[1]
    ds_ = y.shape[1]
    row = lambda w: pl.BlockSpec((tile, w), lambda i: (i, 0))
    return pl.pallas_call(
        _mix_out_kernel,
        grid=(n // tile,),
        in_specs=[row(da), row(ds_), row(d), _const_spec(wglu.shape), _const_spec((1, ds_)),
                  _const_spec((1, da)), _const_spec((1, ds_)), _const_spec(wout.shape), _const_spec((1, d))],
        out_specs=[row(d), pl.BlockSpec((d, tile), lambda i: (0, i))],
        out_shape=[jax.ShapeDtypeStruct((n, d), F32), jax.ShapeDtypeStruct((d, n), BF16)],
        compiler_params=_cparams(("parallel",)),
        name="mix_out",
    )(att, y, x, wglu, bglu.reshape(1, ds_), gatt.reshape(1, da), gssm.reshape(1, ds_), wout,
      gffn.reshape(1, d))


def _extract_topk(s, k):
    n, w = s.shape
    rowi = lax.broadcasted_iota(jnp.int32, (n, w), 0).astype(F32)
    rank = jnp.full((n, w), float(k), F32)
    vals = []
    for r in range(k):
        m = jnp.max(s, axis=0, keepdims=True)
        idx = jnp.min(jnp.where(s == m, rowi, float(n)), axis=0, keepdims=True)
        hit = rowi == idx
        s = jnp.where(hit, NEG_INF, s)
        rank = jnp.where(hit, float(r), rank)
        vals.append(m)
    return rank, vals


_CAND_HEAD = SUBLANES
_CAND_PIECES = [PEER_TOPK] + [SUBLANES] * (_CAND_HEAD - 1)


def _route_chunk(s1, s2):
    k = PEER_TOPK
    rank1, v1 = _extract_topk(s1, k)
    rank2, v2 = _extract_topk(s2, k)
    v1_tail = jnp.concatenate(v1[_CAND_HEAD:], axis=0)
    v2_all = jnp.concatenate(v2, axis=0)
    pieces = [v1[r1] + v2_all[:nr, :] for r1, nr in enumerate(_CAND_PIECES)]
    pieces.append(v1_tail + v2[0])
    cand = jnp.concatenate(pieces, axis=0)
    rank_c, _ = _extract_topk(cand, k)
    sel = (rank_c < float(k)).astype(F32)
    m1, m2 = v1[0], v2[0]
    zsum = jnp.sum(sel * jnp.exp(cand - (m1 + m2)), axis=0, keepdims=True)
    counts, off = [], 0
    for nr in _CAND_PIECES:
        counts.append(jnp.sum(sel[off:off + nr, :], axis=0, keepdims=True))
        off += nr
    tail = sel[off:, :]
    na = jnp.zeros_like(s1)
    for r1 in range(k):
        n_r = counts[r1] if r1 < _CAND_HEAD else tail[r1 - _CAND_HEAD:r1 - _CAND_HEAD + 1, :]
        na = jnp.where(rank1 == float(r1), n_r, na)
    return rank2, jnp.exp(s2 - m2), na, jnp.exp(s1 - m1) / zsum


def _peer_route_kernel(ct_ref, wq_ref, sk_ref, r2_ref, e2_ref, na_ref, fa_ref, qt_ref, *, n_heads):
    tt = ct_ref.shape[1]
    dk = N_KEYS
    qt_ref[...] = jnp.dot(wq_ref[...], ct_ref[...], preferred_element_type=F32)

    def head(h, _):
        q0 = pl.multiple_of(h * 2 * dk, 2 * dk)
        o0 = pl.multiple_of(h * dk, dk)
        s1 = jnp.dot(sk_ref[0], qt_ref[pl.ds(q0, dk), :], preferred_element_type=F32,
                     precision=lax.Precision.HIGHEST)
        s2 = jnp.dot(sk_ref[1], qt_ref[pl.ds(q0 + dk, dk), :], preferred_element_type=F32,
                     precision=lax.Precision.HIGHEST)
        for c in range(tt // LANES):
            ls = slice(c * LANES, (c + 1) * LANES)
            rank2, e2, na, fa = _route_chunk(s1[:, ls], s2[:, ls])
            r2_ref[pl.ds(o0, dk), ls] = rank2.astype(r2_ref.dtype)
            e2_ref[pl.ds(o0, dk), ls] = e2.astype(e2_ref.dtype)
            na_ref[pl.ds(o0, dk), ls] = na
            fa_ref[pl.ds(o0, dk), ls] = fa
        return 0

    lax.fori_loop(0, n_heads, head, 0)


def peer_route(ct, wq_t, subkeys, tile):
    d, n = ct.shape
    dq = wq_t.shape[0]
    n_heads = dq // (2 * N_KEYS)
    rows = n_heads * N_KEYS
    assert tile % LANES == 0
    col = lambda r: pl.BlockSpec((r, tile), lambda i: (0, i))
    return pl.pallas_call(
        functools.partial(_peer_route_kernel, n_heads=n_heads),
        grid=(n // tile,),
        in_specs=[col(d), _const_spec(wq_t.shape), _const_spec(subkeys.shape)],
        out_specs=[col(rows)] * 4,
        out_shape=[jax.ShapeDtypeStruct((rows, n), BF16), jax.ShapeDtypeStruct((rows, n), BF16),
                   jax.ShapeDtypeStruct((rows, n), F32), jax.ShapeDtypeStruct((rows, n), F32)],
        scratch_shapes=[pltpu.VMEM((dq, tile), F32)],
        compiler_params=_cparams(("parallel",)),
        name="peer_route",
    )(ct, wq_t, subkeys.astype(F32))


A_GROUP = 4


def _peer_dense_kernel(ct_ref, up_ref, dn_ref, r2_in, e2_in, na_ref, fa_ref, o_ref, acc_ref, h_ref,
                       r2_ref, e2_ref, *, n_heads):
    j = pl.program_id(1)
    te = up_ref.shape[0]
    dk = N_KEYS

    @pl.when(j == 0)
    def _():
        acc_ref[...] = jnp.zeros_like(acc_ref)
        r2_ref[...] = r2_in[...]
        e2_ref[...] = e2_in[...]

        h_ref[1] = jnp.zeros(h_ref.shape[1:], BF16)

    tt = ct_ref.shape[1]
    n_blk = pl.num_programs(1) - 1
    cur = j % 2

    def down_proj():
        acc_ref[...] += jnp.dot(dn_ref[...], h_ref[1 - cur], preferred_element_type=F32)

    @pl.when(j < n_blk)
    def _():
        down_proj()
        zero = jnp.zeros((), BF16)
        pk = (dk // BF16_ROWS, BF16_ROWS, LANES)

        def packed_row(ref, row, ls):
            return jnp.broadcast_to(ref[pl.ds(row, 1), :][:, ls], (BF16_ROWS, LANES)).astype(BF16)[None]

        g = _gelu(jnp.dot(up_ref[...], ct_ref[...], preferred_element_type=F32)).astype(BF16)
        for k0 in range(0, te // dk, A_GROUP):
            a0 = j * (te // dk) + k0
            for c in range(tt // LANES):
                ls = slice(c * LANES, (c + 1) * LANES)
                ws = [None] * A_GROUP
                for h in range(n_heads):
                    r2 = r2_ref[h * dk:(h + 1) * dk, ls].reshape(pk)
                    e2 = e2_ref[h * dk:(h + 1) * dk, ls].reshape(pk)
                    for kk in range(A_GROUP):
                        n_row = packed_row(na_ref, h * dk + a0 + kk, ls)
                        f_row = packed_row(fa_ref, h * dk + a0 + kk, ls)
                        term = jnp.where(r2 < n_row, e2, zero) * f_row
                        ws[kk] = term if ws[kk] is None else ws[kk] + term
                for kk in range(A_GROUP):
                    rs = slice((k0 + kk) * dk, (k0 + kk + 1) * dk)
                    h_ref[cur, rs, ls] = ws[kk].reshape(dk, LANES) * g[rs, ls]

    @pl.when(j == n_blk)
    def _():
        down_proj()
        o_ref[...] = acc_ref[...].T


def peer_dense(ct, up_bf16, down_t_bf16, r2, e2, na, fa, tile, te):
    d, n = ct.shape
    ne = up_bf16.shape[0]
    rows = r2.shape[0]
    n_heads = rows // N_KEYS
    n_blk = ne // te
    assert tile % LANES == 0 and (te // N_KEYS) % A_GROUP == 0
    col = lambda r: pl.BlockSpec((r, tile), lambda i, j: (0, i))
    return pl.pallas_call(
        functools.partial(_peer_dense_kernel, n_heads=n_heads),
        grid=(n // tile, n_blk + 1),
        in_specs=[col(d),
                  pl.BlockSpec((te, d), lambda i, j: (jnp.minimum(j, n_blk - 1), 0)),
                  pl.BlockSpec((d, te), lambda i, j: (0, jnp.maximum(j - 1, 0))),
                  col(rows), col(rows), col(rows), col(rows)],
        out_specs=pl.BlockSpec((tile, d), lambda i, j: (i, 0)),
        out_shape=jax.ShapeDtypeStruct((n, d), F32),
        scratch_shapes=[pltpu.VMEM((d, tile), F32), pltpu.VMEM((2, te, tile), BF16),
                        pltpu.VMEM((rows, tile), BF16), pltpu.VMEM((rows, tile), BF16)],
        compiler_params=_cparams(("parallel", "arbitrary")),
        name="peer_dense",
    )(ct, up_bf16, down_t_bf16, r2, e2, na, fa)


def _ple_final_kernel(h1_ref, peer_ref, p_ref, gple_ref, wgate_ref, wple_ref, gfin_ref, y_ref):
    h2 = h1_ref[...] + peer_ref[...]
    r = _rms(h2, gple_ref[...]).astype(BF16)
    gate = jax.nn.sigmoid(jnp.dot(r, wgate_ref[...], preferred_element_type=F32))
    h3 = h2 + jnp.dot(p_ref[...].astype(BF16), wple_ref[...], preferred_element_type=F32) * gate
    y_ref[...] = _rms(h3, gfin_ref[...])


def ple_final(h1, peer, p, gple, wgate, wple, gfin, tile):
    n, d = h1.shape
    dp = p.shape[1]
    row = lambda w: pl.BlockSpec((tile, w), lambda i: (i, 0))
    return pl.pallas_call(
        _ple_final_kernel,
        grid=(n // tile,),
        in_specs=[row(d), row(d), row(dp), _const_spec((1, d)), _const_spec(wgate.shape),
                  _const_spec(wple.shape), _const_spec((1, d))],
        out_specs=row(d),
        out_shape=jax.ShapeDtypeStruct((n, d), F32),
        compiler_params=_cparams(("parallel",)),
        name="ple_final",
    )(h1, peer, p, gple.reshape(1, d), wgate, wple, gfin.reshape(1, d))


def _pick_tile(n, pref):
    t = min(n, pref)
    while n % t:
        t //= 2
    return t


def _layer(x, p, w, att_fn, h0_re, h0_im, batch):
    n, d = x.shape
    tile = _pick_tile(n, 256)
    q, k, v, u = in_proj(x, w["g_mix"], w["w_in"], w["widths"], tile)
    att = att_fn(q, k, v)
    l = n // batch
    y, ht_re, ht_im = ssm(u, h0_re, h0_im, w["ssm"], batch, _pick_tile(l, 256), w["ssm_gs"])
    h1, ct = mix_out(att, y, x, w["w_glu"], w["b_glu"], w["g_att_out"], w["g_ssm_out"], w["w_out"],
                     w["g_ffn"], tile)
    r2, e2, na, fa = peer_route(ct, w["w_peer_q_t"], w["peer_subkeys"], tile)
    peer = peer_dense(ct, w["peer_up"], w["peer_down_t"], r2, e2, na, fa, _pick_tile(n, 512), 512)
    y_out = ple_final(h1, peer, p, w["g_ple"], w["w_ple_gate"], w["w_ple"], w["g_final"], tile)
    return y_out, k, v, ht_re, ht_im


def kernel(x_prompt, x_sample, p_prompt, p_sample, cache_k, cache_v, state_ssm_re, state_ssm_im, page_table, g_mix, w_in, att_bias, ssm_a_re, ssm_a_im, ssm_log_step, ssm_b_re, ssm_b_im, ssm_c_re, ssm_c_im, ssm_d, w_glu, b_glu, g_att_out, g_ssm_out, w_out, g_ffn, w_peer_q, peer_subkeys, peer_up, peer_down, w_ple, g_ple, w_ple_gate, g_final):
    depth = w_in.shape[0]
    assert depth == 1
    bsz, seq, d = x_prompt.shape
    n_seq, dec_seq, _ = x_sample.shape
    nh = cache_k.shape[3]
    d_att = nh * D_HEAD
    n_grp = ssm_a_re.shape[1]
    d_ssm = n_grp * SSM_GROUP
    gs = 16
    l = 0
    w = dict(
        g_mix=g_mix[l], w_in=w_in[l].astype(BF16), widths=(d_att, d_att, d_att, d_ssm),
        ssm=_ssm_params(ssm_a_re[l], ssm_a_im[l], ssm_log_step[l], ssm_b_re[l], ssm_b_im[l],
                        ssm_c_re[l], ssm_c_im[l], ssm_d[l], gs),
        ssm_gs=gs,
        w_glu=w_glu[l].astype(BF16), b_glu=b_glu[l], g_att_out=g_att_out[l], g_ssm_out=g_ssm_out[l],
        w_out=w_out[l].astype(BF16), g_ffn=g_ffn[l],
        w_peer_q_t=w_peer_q[l].T.astype(BF16), peer_subkeys=peer_subkeys[l],
        peer_up=peer_up[l].astype(BF16), peer_down_t=peer_down[l].T.astype(BF16),
        w_ple=w_ple[l].astype(BF16), g_ple=g_ple[l], w_ple_gate=w_ple_gate[l].astype(BF16),
        g_final=g_final,
    )
    bias = att_bias[l].astype(F32)

    zeros_state = jnp.zeros((bsz, n_grp * SSM_STATE), F32)
    att_p = lambda q, k, v: sb_attention_prompt(q, k, v, bias, bsz, _pick_tile(seq, 256))
    y_p, k_p, v_p, r_p, i_p = _layer(x_prompt.reshape(bsz * seq, d), p_prompt[l].reshape(bsz * seq, -1),
                                     w, att_p, zeros_state, zeros_state, bsz)

    att_s = lambda q, k, v: sb_attention_sample(q, k, v, cache_k, cache_v, l, page_table, bias,
                                                _pick_tile(page_table.shape[1], 8))
    y_s, k_s, v_s, r_s, i_s = _layer(x_sample.reshape(n_seq * dec_seq, d),
                                     p_sample[l].reshape(n_seq * dec_seq, -1), w, att_s,
                                     state_ssm_re[l].reshape(n_seq, -1), state_ssm_im[l].reshape(n_seq, -1),
                                     n_seq)

    kv = lambda a, b_, t: a.reshape(1, b_, t, nh, D_HEAD).astype(cache_k.dtype)
    st = lambda a, b_: a.reshape(1, b_, n_grp, SSM_STATE).astype(state_ssm_re.dtype)
    return (y_p.reshape(bsz, seq, d), y_s.reshape(n_seq, dec_seq, d),
            kv(k_p, bsz, seq), kv(v_p, bsz, seq), st(r_p, bsz), st(i_p, bsz),
            kv(k_s, n_seq, dec_seq), kv(v_s, n_seq, dec_seq), st(r_s, n_seq), st(i_s, n_seq))
```
